```python
import math
import jax
import jax.numpy as jnp
from jax import lax
import numpy as np

D_MODEL = 4096
BATCH = 2
SEQ = 4096
DEPTH = 2

HEAD_DIM = 128
MIXER_HEADS = D_MODEL // HEAD_DIM // 4
S5_WIDTH = MIXER_HEADS * HEAD_DIM
S5_GROUP = 16
S5_GROUPS = S5_WIDTH // S5_GROUP
S5_STATE = 64
S5_DT_MIN = 0.001
S5_DT_MAX = 0.1
GDN_HEADS = MIXER_HEADS
GDN_WIDTH = GDN_HEADS * HEAD_DIM
GDN_CONV = 4
GDN_CHUNK = 64
DSA_PATTERNS = ((128, 1), (512, 4), (2048, 16))
DSA_HEADS_PER_GROUP = MIXER_HEADS // 2
DSA_HEADS = DSA_HEADS_PER_GROUP * len(DSA_PATTERNS)
DSA_WIDTH = DSA_HEADS * HEAD_DIM
DSA_OUT = DSA_HEADS_PER_GROUP * HEAD_DIM
DSA_BLOCK = 128
FOX_HEADS = MIXER_HEADS
FOX_WIDTH = FOX_HEADS * HEAD_DIM
FOX_BLOCK = 128
N_BRANCHES = 4
ROPE_THETA = 10000.0
NORM_EPS = 1e-6
N_MOD = 6
FFN_DIM = 256 * ((8 * D_MODEL // 3 + 255) // 256)
N_EXPERTS = 8
TOP_K = 2
EXPERT_DIM = 7 * D_MODEL // 8
N_DENSE = (DEPTH + 1) // 2
N_MOE = DEPTH // 2
IN_SPLITS = (S5_WIDTH,
             GDN_WIDTH, GDN_WIDTH, GDN_WIDTH, GDN_HEADS, GDN_HEADS, GDN_WIDTH,
             DSA_WIDTH, DSA_WIDTH, DSA_WIDTH,
             FOX_WIDTH, FOX_WIDTH, FOX_WIDTH, FOX_HEADS,
             N_BRANCHES * D_MODEL)
IN_WIDTH = sum(IN_SPLITS)

F32 = jnp.float32

kernel_name = 'hybrid_gated_s5_gdn_dilated_fox_moe'


def _split(z, sizes):
    out, start = [], 0
    for s in sizes:
        out.append(z[..., start:start + s])
        start += s
    return out


def _rmsnorm(x, w):
    xf = x.astype(F32)
    y = xf * lax.rsqrt(jnp.mean(xf * xf, axis=-1, keepdims=True) + NORM_EPS)
    return (y * w.astype(F32)).astype(x.dtype)


def _l2norm(x):
    xf = x.astype(F32)
    return xf * lax.rsqrt(jnp.sum(xf * xf, axis=-1, keepdims=True) + NORM_EPS)


def _rope(x, positions):
    half = HEAD_DIM // 2
    inv_freq = jnp.power(ROPE_THETA, -jnp.arange(half, dtype=F32) * (2.0 / HEAD_DIM))
    ang = positions.astype(F32)[..., None] * inv_freq
    cos = jnp.cos(ang)[:, :, None, :]
    sin = jnp.sin(ang)[:, :, None, :]
    xf = x.astype(F32)
    x1, x2 = xf[..., :half], xf[..., half:]
    return jnp.concatenate([x1 * cos - x2 * sin, x2 * cos + x1 * sin], axis=-1)


def _complex_scan_op(e1, e2):
    a1r, a1i, b1r, b1i = e1
    a2r, a2i, b2r, b2i = e2
    return (a2r * a1r - a2i * a1i,
            a2r * a1i + a2i * a1r,
            a2r * b1r - a2i * b1i + b2r,
            a2r * b1i + a2i * b1r + b2i)


def _s5_mixer(u, lam_re, lam_im, b_re, b_im, c_re, c_im, d_skip, log_dt, w_glu):
    bsz, L, _ = u.shape
    uf = u.astype(F32).reshape(bsz, L, S5_GROUPS, S5_GROUP)
    lr, li = lam_re.astype(F32), lam_im.astype(F32)
    dt = jnp.exp(log_dt.astype(F32))[:, None]
    mag = jnp.exp(lr * dt)
    abar_re, abar_im = mag * jnp.cos(li * dt), mag * jnp.sin(li * dt)
    den = lr * lr + li * li
    num_re, num_im = abar_re - 1.0, abar_im
    coef_re = (num_re * lr + num_im * li) / den
    coef_im = (num_im * lr - num_re * li) / den
    br, bi = b_re.astype(F32), b_im.astype(F32)
    bbar_re = coef_re[..., None] * br - coef_im[..., None] * bi
    bbar_im = coef_re[..., None] * bi + coef_im[..., None] * br
    bu_re = jnp.einsum('blgp,gnp->blgn', uf, bbar_re)
    bu_im = jnp.einsum('blgp,gnp->blgn', uf, bbar_im)
    a_re = jnp.broadcast_to(abar_re, (1, L) + abar_re.shape)
    a_im = jnp.broadcast_to(abar_im, (1, L) + abar_im.shape)
    _, _, s_re, s_im = lax.associative_scan(_complex_scan_op, (a_re, a_im, bu_re, bu_im), axis=1)
    y = (jnp.einsum('blgn,gpn->blgp', s_re, c_re.astype(F32))
         - jnp.einsum('blgn,gpn->blgp', s_im, c_im.astype(F32))
         + d_skip.astype(F32) * uf)
    y = jax.nn.gelu(y.reshape(bsz, L, S5_WIDTH))
    y = y * jax.nn.sigmoid(y @ w_glu.astype(F32))
    return y.astype(u.dtype)


def _causal_conv(x, w):
    k = w.shape[0]
    return lax.conv_general_dilated(
        x, w.astype(x.dtype)[:, None, :], window_strides=(1,), padding=[(k - 1, 0)],
        dimension_numbers=('NWC', 'WIO', 'NWC'), feature_group_count=x.shape[-1])


def _gated_delta_chunked(q, k, v, g, beta):
    bsz, nh, L, dk = q.shape
    dv = v.shape[-1]
    cs = GDN_CHUNK
    n = L // cs
    q = (q * dk ** -0.5).reshape(bsz, nh, n, cs, dk)
    k = k.reshape(bsz, nh, n, cs, dk)
    v = v.reshape(bsz, nh, n, cs, dv)
    beta = beta.reshape(bsz, nh, n, cs)
    gc = jnp.cumsum(g.reshape(bsz, nh, n, cs), axis=-1)
    idx = jnp.arange(cs)
    incl = idx[:, None] >= idx[None, :]
    strict = idx[:, None] > idx[None, :]
    diff = gc[..., :, None] - gc[..., None, :]
    decay = jnp.where(incl, jnp.exp(jnp.where(incl, diff, 0.0)), 0.0)
    kb = k * beta[..., None]
    m = jnp.where(strict, jnp.einsum('bhnid,bhnjd->bhnij', kb, k) * decay, 0.0)
    eye = jnp.eye(cs, dtype=F32)
    t = lax.linalg.triangular_solve(eye + m, jnp.broadcast_to(eye, m.shape),
                                    left_side=True, lower=True)
    u = jnp.einsum('bhnij,bhnjd->bhnid', t, v * beta[..., None])
    w = jnp.einsum('bhnij,bhnjd->bhnid', t, kb * jnp.exp(gc)[..., None])
    attn = jnp.einsum('bhnid,bhnjd->bhnij', q, k) * decay
    qg = q * jnp.exp(gc)[..., None]
    kg = k * jnp.exp(gc[..., -1:] - gc)[..., None]
    glast = jnp.exp(gc[..., -1])

    def step(state, xs):
        u_c, w_c, attn_c, qg_c, kg_c, gl_c = xs
        v_new = u_c - jnp.einsum('bhcd,bhde->bhce', w_c, state)
        o = (jnp.einsum('bhcd,bhde->bhce', qg_c, state)
             + jnp.einsum('bhcs,bhse->bhce', attn_c, v_new))
        state = state * gl_c[..., None, None] + jnp.einsum('bhcd,bhce->bhde', kg_c, v_new)
        return state, o

    xs = tuple(jnp.moveaxis(a, 2, 0) for a in (u, w, attn, qg, kg, glast))
    s0 = jnp.zeros((bsz, nh, dk, dv), F32)
    _, o = lax.scan(step, s0, xs)
    return jnp.moveaxis(o, 0, 2).reshape(bsz, nh, L, dv)


def _gdn_mixer(q, k, v, a, b, gate, conv_w, a_log, dt_bias, norm_w):
    dtype = q.dtype
    bsz, L, _ = q.shape
    qkv = jax.nn.silu(_causal_conv(jnp.concatenate([q, k, v], axis=-1), conv_w))
    q, k, v = jnp.split(qkv, 3, axis=-1)

    def heads(t):
        return t.reshape(bsz, L, GDN_HEADS, HEAD_DIM).transpose(0, 2, 1, 3)

    qh, kh = _l2norm(heads(q)), _l2norm(heads(k))
    vh = heads(v).astype(F32)
    beta = jax.nn.sigmoid(b.astype(F32)).transpose(0, 2, 1)
    g = (-jnp.exp(a_log.astype(F32))
         * jax.nn.softplus(a.astype(F32) + dt_bias.astype(F32))).transpose(0, 2, 1)
    o = _gated_delta_chunked(qh, kh, vh, g, beta).transpose(0, 2, 1, 3)
    o = _rmsnorm(o, norm_w) * jax.nn.silu(gate.astype(F32)).reshape(bsz, L, GDN_HEADS, HEAD_DIM)
    return o.reshape(bsz, L, GDN_WIDTH).astype(dtype)


def _banded_attention(q, k, v, window):
    nn_, s_len, nh, d = q.shape
    bq = math.gcd(s_len, DSA_BLOCK)
    nb = s_len // bq
    kp = jnp.pad(k, ((0, 0), (window, 0), (0, 0), (0, 0)))
    vp = jnp.pad(v, ((0, 0), (window, 0), (0, 0), (0, 0)))
    starts = jnp.arange(nb) * bq
    kidx = starts[:, None] + jnp.arange(bq + window)[None, :]
    kb, vb = kp[:, kidx], vp[:, kidx]
    qb = q.reshape(nn_, nb, bq, nh, d)
    s = jnp.einsum('nbqhd,nbkhd->nbhqk', qb, kb).astype(F32) * (d ** -0.5)
    qi = jnp.arange(bq)[:, None]
    kj = jnp.arange(bq + window)[None, :]
    band = (kj >= qi) & (kj <= qi + window)
    valid = kj[None] >= (window - starts)[:, None, None]
    mask = band[None] & valid
    s = jnp.where(mask[None, :, None], s, -jnp.inf)
    mx = jnp.max(s, axis=-1, keepdims=True)
    p = jnp.exp(s - mx)
    den = jnp.sum(p, axis=-1, keepdims=True)
    o = jnp.einsum('nbhqk,nbkhd->nbqhd', p / den, vb.astype(F32))
    lse = (mx + jnp.log(den))[..., 0].transpose(0, 1, 3, 2)
    return o.reshape(nn_, s_len, nh, d), lse.reshape(nn_, s_len, nh)


def _dsa_mixer(q, k, v, positions, q_norm, k_norm):
    bsz, L, _ = q.shape
    hg = DSA_HEADS_PER_GROUP
    shp = (bsz, L, DSA_HEADS, HEAD_DIM)
    qh = _rope(_rmsnorm(q.reshape(shp), q_norm), positions)
    kh = _rope(_rmsnorm(k.reshape(shp), k_norm), positions)
    vh = v.reshape(shp).astype(F32)
    outs, lses = [], []
    for gi, (window, dil) in enumerate(DSA_PATTERNS):
        lo, hi = gi * hg, (gi + 1) * hg
        ls = L // dil

        def to_sub(t):
            t = t[:, :, lo:hi].reshape(bsz, ls, dil, hg, HEAD_DIM).transpose(0, 2, 1, 3, 4)
            return t.reshape(bsz * dil, ls, hg, HEAD_DIM)

        o, lse = _banded_attention(to_sub(qh), to_sub(kh), to_sub(vh), window // dil)
        outs.append(o.reshape(bsz, dil, ls, hg, HEAD_DIM).transpose(0, 2, 1, 3, 4)
                    .reshape(bsz, L, hg, HEAD_DIM))
        lses.append(lse.reshape(bsz, dil, ls, hg).transpose(0, 2, 1, 3).reshape(bsz, L, hg))
    wts = jax.nn.softmax(jnp.stack(lses, axis=0), axis=0)
    o = jnp.einsum('gblh,gblhd->blhd', wts, jnp.stack(outs, axis=0))
    return o.reshape(bsz, L, DSA_OUT).astype(q.dtype)


def _fox_mixer(q, k, v, f_logit, q_norm, k_norm, f_bias):
    bsz, L, _ = q.shape
    shp = (bsz, L, FOX_HEADS, HEAD_DIM)
    qh = _rmsnorm(q.reshape(shp), q_norm).astype(F32) * (HEAD_DIM ** -0.5)
    kh = _rmsnorm(k.reshape(shp), k_norm).astype(F32)
    vh = v.reshape(shp).astype(F32)
    logf = jax.nn.log_sigmoid(f_logit.astype(F32) + f_bias.astype(F32))
    cum = jnp.cumsum(logf, axis=1).transpose(0, 2, 1)
    nb = L // FOX_BLOCK
    qb = qh.reshape(bsz, nb, FOX_BLOCK, FOX_HEADS, HEAD_DIM).transpose(1, 0, 2, 3, 4)
    cq = cum.reshape(bsz, FOX_HEADS, nb, FOX_BLOCK).transpose(2, 0, 1, 3)
    kpos = jnp.arange(L)

    def block(args):
        q_i, c_i, b_i = args
        qpos = b_i * FOX_BLOCK + jnp.arange(FOX_BLOCK)
        s = jnp.einsum('bqhd,bkhd->bhqk', q_i, kh) + c_i[..., None] - cum[:, :, None, :]
        s = jnp.where(kpos[None, :] <= qpos[:, None], s, -jnp.inf)
        p = jax.nn.softmax(s, axis=-1)
        return jnp.einsum('bhqk,bkhd->bqhd', p, vh)

    o = lax.map(block, (qb, cq, jnp.arange(nb)))
    return o.transpose(1, 0, 2, 3, 4).reshape(bsz, L, FOX_WIDTH).astype(q.dtype)


def _token_mixer(h, positions, w_in, lam_re, lam_im, b_re, b_im, c_re, c_im, d_skip, log_dt, w_glu,
                 conv_w, a_log, dt_bias, gdn_norm_w, dsa_qn, dsa_kn, fox_qn, fox_kn, f_bias,
                 w_br_s5, w_br_gdn, w_br_dsa, w_br_fox, w_out):
    bsz, L, _ = h.shape
    z = h @ w_in
    (s5_u, g_q, g_k, g_v, g_a, g_b, g_gate, d_q, d_k, d_v,
     f_q, f_k, f_v, f_f, br_gate) = _split(z, IN_SPLITS)
    y_s5 = _s5_mixer(s5_u, lam_re, lam_im, b_re, b_im, c_re, c_im, d_skip, log_dt, w_glu)
    y_gdn = _gdn_mixer(g_q, g_k, g_v, g_a, g_b, g_gate, conv_w, a_log, dt_bias, gdn_norm_w)
    y_dsa = _dsa_mixer(d_q, d_k, d_v, positions, dsa_qn, dsa_kn)
    y_fox = _fox_mixer(f_q, f_k, f_v, f_f, fox_qn, fox_kn, f_bias)
    gate = jax.nn.sigmoid(br_gate.astype(F32)).reshape(bsz, L, N_BRANCHES, D_MODEL)
    merged = (gate[:, :, 0] * (y_s5 @ w_br_s5).astype(F32)
              + gate[:, :, 1] * (y_gdn @ w_br_gdn).astype(F32)
              + gate[:, :, 2] * (y_dsa @ w_br_dsa).astype(F32)
              + gate[:, :, 3] * (y_fox @ w_br_fox).astype(F32))
    return merged.astype(h.dtype) @ w_out


def _swiglu(h, wg, wu, wd):
    return (jax.nn.silu(h @ wg) * (h @ wu)) @ wd


def _moe(h, router, wg, wu, wd):
    logits = (h @ router).astype(F32)
    top_v, top_i = lax.top_k(logits, TOP_K)
    top_w = jax.nn.softmax(top_v, axis=-1)
    gate = jnp.sum(jax.nn.one_hot(top_i, N_EXPERTS, dtype=F32) * top_w[..., None], axis=-2)
    out = jnp.zeros(h.shape, F32)
    for e in range(N_EXPERTS):
        out = out + gate[..., e:e + 1] * _swiglu(h, wg[e], wu[e], wd[e]).astype(F32)
    return out.astype(h.dtype)


def setup_inputs(seed: int = 0) -> dict:
    key = jax.random.key(seed)
    ks = iter(jax.random.split(key, 48))

    def nrm(shape, scale):
        return jax.random.normal(next(ks), shape, F32) * scale

    def gain(shape):
        return 1.0 + 0.02 * jax.random.normal(next(ks), shape, F32)

    G, N, P = S5_GROUPS, S5_STATE, S5_GROUP
    x = nrm((BATCH, SEQ, D_MODEL), 1.0)
    c = nrm((BATCH, D_MODEL), 1.0)
    positions = (jnp.arange(SEQ, dtype=jnp.int32)[None, :]
                 + jax.random.randint(next(ks), (BATCH, 1), 0, SEQ, dtype=jnp.int32))
    w_mod = nrm((D_MODEL, N_MOD * D_MODEL), 0.5 * D_MODEL ** -0.5)
    b_mod = nrm((DEPTH, N_MOD, D_MODEL), 0.1)
    norm_mix_w = gain((DEPTH, D_MODEL))
    norm_ffn_w = gain((DEPTH, D_MODEL))
    w_in = nrm((DEPTH, D_MODEL, IN_WIDTH), D_MODEL ** -0.5)
    s5_lambda_re = -0.5 + nrm((DEPTH, G, N), 0.01)
    s5_lambda_im = jnp.pi * jnp.arange(N, dtype=F32) + nrm((DEPTH, G, N), 0.01)
    s5_b_re = nrm((DEPTH, G, N, P), (2 * P) ** -0.5)
    s5_b_im = nrm((DEPTH, G, N, P), (2 * P) ** -0.5)
    s5_c_re = nrm((DEPTH, G, P, N), (2 * N) ** -0.5)
    s5_c_im = nrm((DEPTH, G, P, N), (2 * N) ** -0.5)
    s5_d = nrm((DEPTH, G, P), 1.0)
    s5_log_dt = jax.random.uniform(next(ks), (DEPTH, G), F32,
                                   math.log(S5_DT_MIN), math.log(S5_DT_MAX))
    s5_w_glu = nrm((DEPTH, S5_WIDTH, S5_WIDTH), S5_WIDTH ** -0.5)
    gdn_conv_w = nrm((DEPTH, GDN_CONV, 3 * GDN_WIDTH), GDN_CONV ** -0.5)
    gdn_a_log = jnp.log(jax.random.uniform(next(ks), (DEPTH, GDN_HEADS), F32, 1.0, 16.0))
    dt = jnp.exp(jax.random.uniform(next(ks), (DEPTH, GDN_HEADS), F32,
                                    math.log(1e-3), math.log(1e-1)))
    gdn_dt_bias = dt + jnp.log(-jnp.expm1(-dt))
    gdn_norm_w = gain((DEPTH, HEAD_DIM))
    dsa_q_norm = gain((DEPTH, HEAD_DIM))
    dsa_k_norm = gain((DEPTH, HEAD_DIM))
    fox_q_norm = gain((DEPTH, HEAD_DIM))
    fox_k_norm = gain((DEPTH, HEAD_DIM))
    fox_f_bias = jax.random.uniform(next(ks), (DEPTH, FOX_HEADS), F32, 1.0, 4.0)
    w_br_s5 = nrm((DEPTH, S5_WIDTH, D_MODEL), S5_WIDTH ** -0.5)
    w_br_gdn = nrm((DEPTH, GDN_WIDTH, D_MODEL), GDN_WIDTH ** -0.5)
    w_br_dsa = nrm((DEPTH, DSA_OUT, D_MODEL), DSA_OUT ** -0.5)
    w_br_fox = nrm((DEPTH, FOX_WIDTH, D_MODEL), FOX_WIDTH ** -0.5)
    w_out = nrm((DEPTH, D_MODEL, D_MODEL), D_MODEL ** -0.5)
    ffn_w_gate = nrm((N_DENSE, D_MODEL, FFN_DIM), D_MODEL ** -0.5)
    ffn_w_up = nrm((N_DENSE, D_MODEL, FFN_DIM), D_MODEL ** -0.5)
    ffn_w_down = nrm((N_DENSE, FFN_DIM, D_MODEL), FFN_DIM ** -0.5)
    moe_router = nrm((N_MOE, D_MODEL, N_EXPERTS), D_MODEL ** -0.5)
    moe_w_gate = nrm((N_MOE, N_EXPERTS, D_MODEL, EXPERT_DIM), D_MODEL ** -0.5)
    moe_w_up = nrm((N_MOE, N_EXPERTS, D_MODEL, EXPERT_DIM), D_MODEL ** -0.5)
    moe_w_down = nrm((N_MOE, N_EXPERTS, EXPERT_DIM, D_MODEL), EXPERT_DIM ** -0.5)
    return {'x': x, 'c': c, 'positions': positions, 'w_mod': w_mod, 'b_mod': b_mod,
            'norm_mix_w': norm_mix_w, 'norm_ffn_w': norm_ffn_w, 'w_in': w_in,
            's5_lambda_re': s5_lambda_re, 's5_lambda_im': s5_lambda_im,
            's5_b_re': s5_b_re, 's5_b_im': s5_b_im, 's5_c_re': s5_c_re, 's5_c_im': s5_c_im,
            's5_d': s5_d, 's5_log_dt': s5_log_dt, 's5_w_glu': s5_w_glu,
            'gdn_conv_w': gdn_conv_w, 'gdn_a_log': gdn_a_log, 'gdn_dt_bias': gdn_dt_bias,
            'gdn_norm_w': gdn_norm_w, 'dsa_q_norm': dsa_q_norm, 'dsa_k_norm': dsa_k_norm,
            'fox_q_norm': fox_q_norm, 'fox_k_norm': fox_k_norm, 'fox_f_bias': fox_f_bias,
            'w_br_s5': w_br_s5, 'w_br_gdn': w_br_gdn, 'w_br_dsa': w_br_dsa, 'w_br_fox': w_br_fox,
            'w_out': w_out, 'ffn_w_gate': ffn_w_gate, 'ffn_w_up': ffn_w_up, 'ffn_w_down': ffn_w_down,
            'moe_router': moe_router, 'moe_w_gate': moe_w_gate, 'moe_w_up': moe_w_up,
            'moe_w_down': moe_w_down}


def reference(x, c, positions, w_mod, b_mod, norm_mix_w, norm_ffn_w, w_in,
              s5_lambda_re, s5_lambda_im, s5_b_re, s5_b_im, s5_c_re, s5_c_im,
              s5_d, s5_log_dt, s5_w_glu, gdn_conv_w, gdn_a_log, gdn_dt_bias,
              gdn_norm_w, dsa_q_norm, dsa_k_norm, fox_q_norm, fox_k_norm, fox_f_bias,
              w_br_s5, w_br_gdn, w_br_dsa, w_br_fox, w_out, ffn_w_gate, ffn_w_up, ffn_w_down,
              moe_router, moe_w_gate, moe_w_up, moe_w_down):
    bsz = x.shape[0]
    cond = (jax.nn.silu(c.astype(F32)) @ w_mod.astype(F32)).reshape(bsz, N_MOD, D_MODEL)
    for l in range(DEPTH):
        mod = (cond + b_mod[l].astype(F32)).astype(x.dtype)[:, :, None, :]
        shift1, scale1, gate1, shift2, scale2, gate2 = (mod[:, i] for i in range(N_MOD))
        h = _rmsnorm(x, norm_mix_w[l]) * (1 + scale1) + shift1
        mix = _token_mixer(h, positions, w_in[l],
                           s5_lambda_re[l], s5_lambda_im[l], s5_b_re[l], s5_b_im[l],
                           s5_c_re[l], s5_c_im[l], s5_d[l], s5_log_dt[l], s5_w_glu[l],
                           gdn_conv_w[l], gdn_a_log[l], gdn_dt_bias[l], gdn_norm_w[l],
                           dsa_q_norm[l], dsa_k_norm[l], fox_q_norm[l], fox_k_norm[l], fox_f_bias[l],
                           w_br_s5[l], w_br_gdn[l], w_br_dsa[l], w_br_fox[l], w_out[l])
        x = x + gate1 * mix
        h = _rmsnorm(x, norm_ffn_w[l]) * (1 + scale2) + shift2
        j = l // 2
        if l % 2 == 0:
            f = _swiglu(h, ffn_w_gate[j], ffn_w_up[j], ffn_w_down[j])
        else:
            f = _moe(h, moe_router[j], moe_w_gate[j], moe_w_up[j], moe_w_down[j])
        x = x + gate2 * f
    return x
```

```python
import functools
import math

import jax
import jax.numpy as jnp
from jax import lax
from jax.experimental import pallas as pl
from jax.experimental.pallas import tpu as pltpu

F32 = jnp.float32
BF16 = jnp.bfloat16

D_MODEL = 4096
HEAD_DIM = 128
MIXER_HEADS = 8
S5_WIDTH = 1024
S5_GROUP = 16
S5_GROUPS = 64
S5_STATE = 64
GDN_HEADS = 8
GDN_WIDTH = 1024
GDN_CHUNK = 64
DSA_PATTERNS = ((128, 1), (512, 4), (2048, 16))
DSA_HEADS_PER_GROUP = 4
DSA_HEADS = 12
DSA_WIDTH = 1536
DSA_OUT = 512
DSA_BLOCK = 128
FOX_HEADS = 8
FOX_WIDTH = 1024
FOX_BLOCK = 128
N_BRANCHES = 4
ROPE_THETA = 10000.0
NORM_EPS = 1e-6
N_MOD = 6
FFN_DIM = 11008
FFN_PAD = 11264
N_EXPERTS = 8
TOP_K = 2
EXPERT_DIM = 3584

OFF_S5 = 0
OFF_GQ = 1024
OFF_GK = 2048
OFF_GV = 3072
OFF_GGATE = 4096
OFF_DQ = 5120
OFF_DK = 6656
OFF_DV = 8192
OFF_FQ = 9728
OFF_FK = 10752
OFF_FV = 11776
OFF_SMALL = 12800
MIX_WIDTH = 13312

VMEM_LIMIT_BYTES = 56 * 1024 * 1024


def _cparams(*sem):
    return pltpu.CompilerParams(dimension_semantics=sem, vmem_limit_bytes=VMEM_LIMIT_BYTES)


def _mm_kernel(a_ref, w_ref, *rest, nk, residual):
    if residual:
        x_ref, g_ref, o_ref, acc_ref = rest
    else:
        o_ref, acc_ref = rest
    k = pl.program_id(2)

    @pl.when(k == 0)
    def _():
        acc_ref[...] = jnp.zeros_like(acc_ref)

    acc_ref[...] += jnp.dot(a_ref[...], w_ref[...], preferred_element_type=F32)

    @pl.when(k == nk - 1)
    def _():
        acc = acc_ref[...]
        if residual:
            acc = x_ref[...] + g_ref[...] * acc
        o_ref[...] = acc.astype(o_ref.dtype)


def _matmul(a, w, *, out_dtype, tm=1024, tn=512, tk=None, x=None, gate=None, rows_per_batch=None):
    m, kdim = a.shape
    n = w.shape[1]
    tk = kdim if tk is None else tk
    nk = kdim // tk
    assert m % tm == 0 and n % tn == 0 and kdim % tk == 0
    residual = x is not None
    in_specs = [pl.BlockSpec((tm, tk), lambda i, j, k: (i, k)),
                pl.BlockSpec((tk, tn), lambda i, j, k: (k, j))]
    args = [a, w]
    if residual:
        tiles_per_batch = rows_per_batch // tm
        in_specs += [pl.BlockSpec((tm, tn), lambda i, j, k: (i, j)),
                     pl.BlockSpec((None, 1, tn), lambda i, j, k: (i // tiles_per_batch, 0, j))]
        args += [x, gate]
    return pl.pallas_call(
        functools.partial(_mm_kernel, nk=nk, residual=residual),
        grid=(m // tm, n // tn, nk),
        in_specs=in_specs,
        out_specs=pl.BlockSpec((tm, tn), lambda i, j, k: (i, j)),
        out_shape=jax.ShapeDtypeStruct((m, n), out_dtype),
        scratch_shapes=[pltpu.VMEM((tm, tn), F32)],
        compiler_params=_cparams("parallel", "parallel", "arbitrary"),
    )(*args)


def _swiglu_kernel(a_ref, wg_ref, wu_ref, *rest, scaled):
    if scaled:
        s_ref, o_ref = rest
    else:
        (o_ref,) = rest
    a = a_ref[...]
    g = jnp.dot(a, wg_ref[...], preferred_element_type=F32)
    u = jnp.dot(a, wu_ref[...], preferred_element_type=F32)
    y = g * jax.nn.sigmoid(g) * u
    if scaled:
        y = y * s_ref[:, 0:1]
    o_ref[...] = y.astype(o_ref.dtype)


def _swiglu_up(a, wg, wu, *, tm=1024, tn=512, row_scale=None, tiles_per_expert=None):
    m, kdim = a.shape
    if wg.ndim == 3:
        n = wg.shape[0] * wg.shape[2]
        w_spec = pl.BlockSpec((None, kdim, tn),
                              lambda i, j: (j // tiles_per_expert, 0, j % tiles_per_expert))
    else:
        n = wg.shape[1]
        w_spec = pl.BlockSpec((kdim, tn), lambda i, j: (0, j))
    in_specs = [pl.BlockSpec((tm, kdim), lambda i, j: (i, 0)), w_spec, w_spec]
    args = [a, wg, wu]
    scaled = row_scale is not None
    if scaled:
        in_specs.append(pl.BlockSpec((tm, 128), lambda i, j: (i, j // tiles_per_expert)))
        args.append(row_scale)
    return pl.pallas_call(
        functools.partial(_swiglu_kernel, scaled=scaled),
        grid=(m // tm, n // tn),
        in_specs=in_specs,
        out_specs=pl.BlockSpec((tm, tn), lambda i, j: (i, j)),
        out_shape=jax.ShapeDtypeStruct((m, n), BF16),
        compiler_params=_cparams("parallel", "parallel"),
    )(*args)


def _merge_kernel(y0, y1, y2, y3, w0, w1, w2, w3, g0, g1, g2, g3, o_ref):
    acc = None
    for y_ref, w_ref, g_ref in ((y0, w0, g0), (y1, w1, g1), (y2, w2, g2), (y3, w3, g3)):
        p = jnp.dot(y_ref[...], w_ref[...], preferred_element_type=F32)
        p = jax.nn.sigmoid(g_ref[...].astype(F32)) * p
        acc = p if acc is None else acc + p
    o_ref[...] = acc.astype(o_ref.dtype)


def _merge(ys, ws, gates, *, tm=1024, tn=512):
    m = ys[0].shape[0]
    n = ws[0].shape[1]
    nj = n // tn
    in_specs = [pl.BlockSpec((tm, y.shape[1]), lambda i, j: (i, 0)) for y in ys]
    in_specs += [pl.BlockSpec((w.shape[0], tn), lambda i, j: (0, j)) for w in ws]
    in_specs += [pl.BlockSpec((tm, tn), functools.partial(lambda i, j, b: (i, b * nj + j), b=b))
                 for b in range(N_BRANCHES)]
    return pl.pallas_call(
        _merge_kernel,
        grid=(m // tm, nj),
        in_specs=in_specs,
        out_specs=pl.BlockSpec((tm, tn), lambda i, j: (i, j)),
        out_shape=jax.ShapeDtypeStruct((m, n), BF16),
        compiler_params=_cparams("parallel", "parallel"),
    )(*ys, *ws, gates, gates, gates, gates)


def _normmod_kernel(x_ref, w_ref, scale_ref, shift_ref, o_ref):
    x = x_ref[...]
    y = x * lax.rsqrt(jnp.mean(x * x, axis=-1, keepdims=True) + NORM_EPS)
    y = y * w_ref[...]
    o_ref[...] = (y * (1.0 + scale_ref[...]) + shift_ref[...]).astype(o_ref.dtype)


def _norm_modulate(x, w, scale, shift, *, rows_per_batch, tm=256, out_dtype=BF16):
    m, d = x.shape
    tiles_per_batch = rows_per_batch // tm
    mod_spec = pl.BlockSpec((None, 1, d), lambda i: (i // tiles_per_batch, 0, 0))
    return pl.pallas_call(
        _normmod_kernel,
        grid=(m // tm,),
        in_specs=[pl.BlockSpec((tm, d), lambda i: (i, 0)),
                  pl.BlockSpec((1, d), lambda i: (0, 0)), mod_spec, mod_spec],
        out_specs=pl.BlockSpec((tm, d), lambda i: (i, 0)),
        out_shape=jax.ShapeDtypeStruct((m, d), out_dtype),
        compiler_params=_cparams("parallel"),
    )(x, w.reshape(1, d), scale, shift)


def _cond_kernel(c_ref, w_ref, o_ref):
    c = c_ref[...]
    a = (c * jax.nn.sigmoid(c)).astype(BF16)
    o_ref[...] = jnp.dot(a, w_ref[...].astype(BF16), preferred_element_type=F32)


def _cond_proj(c_pad, w_mod, *, tn=512):
    d, n = w_mod.shape
    return pl.pallas_call(
        _cond_kernel,
        grid=(n // tn,),
        in_specs=[pl.BlockSpec((8, d), lambda j: (0, 0)), pl.BlockSpec((d, tn), lambda j: (0, j))],
        out_specs=pl.BlockSpec((8, tn), lambda j: (0, j)),
        out_shape=jax.ShapeDtypeStruct((8, n), F32),
        compiler_params=_cparams("parallel"),
    )(c_pad, w_mod)


def _rmsnorm(x, w):
    y = x * lax.rsqrt(jnp.mean(x * x, axis=-1, keepdims=True) + NORM_EPS)
    return y * w


def _l2norm(x):
    return x * lax.rsqrt(jnp.sum(x * x, axis=-1, keepdims=True) + NORM_EPS)


def _rope(x, positions):
    half = HEAD_DIM // 2
    inv_freq = jnp.power(ROPE_THETA, -jnp.arange(half, dtype=F32) * (2.0 / HEAD_DIM))
    ang = positions.astype(F32)[..., None] * inv_freq
    cos = jnp.cos(ang)[:, :, None, :]
    sin = jnp.sin(ang)[:, :, None, :]
    x1, x2 = x[..., :half], x[..., half:]
    return jnp.concatenate([x1 * cos - x2 * sin, x2 * cos + x1 * sin], axis=-1)


def _complex_scan_op(e1, e2):
    a1r, a1i, b1r, b1i = e1
    a2r, a2i, b2r, b2i = e2
    return (a2r * a1r - a2i * a1i, a2r * a1i + a2i * a1r,
            a2r * b1r - a2i * b1i + b2r, a2r * b1i + a2i * b1r + b2i)


def _s5_mixer(u, lam_re, lam_im, b_re, b_im, c_re, c_im, d_skip, log_dt, w_glu):
    bsz, L, _ = u.shape
    uf = u.reshape(bsz, L, S5_GROUPS, S5_GROUP)
    lr, li = lam_re, lam_im
    dt = jnp.exp(log_dt)[:, None]
    mag = jnp.exp(lr * dt)
    abar_re, abar_im = mag * jnp.cos(li * dt), mag * jnp.sin(li * dt)
    den = lr * lr + li * li
    num_re, num_im = abar_re - 1.0, abar_im
    coef_re = (num_re * lr + num_im * li) / den
    coef_im = (num_im * lr - num_re * li) / den
    bbar_re = coef_re[..., None] * b_re - coef_im[..., None] * b_im
    bbar_im = coef_re[..., None] * b_im + coef_im[..., None] * b_re
    bu_re = jnp.einsum('blgp,gnp->blgn', uf, bbar_re)
    bu_im = jnp.einsum('blgp,gnp->blgn', uf, bbar_im)
    a_re = jnp.broadcast_to(abar_re, (1, L) + abar_re.shape)
    a_im = jnp.broadcast_to(abar_im, (1, L) + abar_im.shape)
    _, _, s_re, s_im = lax.associative_scan(_complex_scan_op, (a_re, a_im, bu_re, bu_im), axis=1)
    y = (jnp.einsum('blgn,gpn->blgp', s_re, c_re) - jnp.einsum('blgn,gpn->blgp', s_im, c_im)
         + d_skip * uf)
    y = jax.nn.gelu(y.reshape(bsz, L, S5_WIDTH))
    return y * jax.nn.sigmoid(y @ w_glu)


def _causal_conv(x, w):
    k = w.shape[0]
    return lax.conv_general_dilated(
        x, w[:, None, :], window_strides=(1,), padding=[(k - 1, 0)],
        dimension_numbers=('NWC', 'WIO', 'NWC'), feature_group_count=x.shape[-1])


def _gated_delta_chunked(q, k, v, g, beta):
    bsz, nh, L, dk = q.shape
    dv = v.shape[-1]
    cs = GDN_CHUNK
    n = L // cs
    q = (q * dk ** -0.5).reshape(bsz, nh, n, cs, dk)
    k = k.reshape(bsz, nh, n, cs, dk)
    v = v.reshape(bsz, nh, n, cs, dv)
    beta = beta.reshape(bsz, nh, n, cs)
    gc = jnp.cumsum(g.reshape(bsz, nh, n, cs), axis=-1)
    idx = jnp.arange(cs)
    incl = idx[:, None] >= idx[None, :]
    strict = idx[:, None] > idx[None, :]
    diff = gc[..., :, None] - gc[..., None, :]
    decay = jnp.where(incl, jnp.exp(jnp.where(incl, diff, 0.0)), 0.0)
    kb = k * beta[..., None]
    m = jnp.where(strict, jnp.einsum('bhnid,bhnjd->bhnij', kb, k) * decay, 0.0)
    eye = jnp.eye(cs, dtype=F32)
    t = lax.linalg.triangular_solve(eye + m, jnp.broadcast_to(eye, m.shape),
                                    left_side=True, lower=True)
    u = jnp.einsum('bhnij,bhnjd->bhnid', t, v * beta[..., None])
    w = jnp.einsum('bhnij,bhnjd->bhnid', t, kb * jnp.exp(gc)[..., None])
    attn = jnp.einsum('bhnid,bhnjd->bhnij', q, k) * decay
    qg = q * jnp.exp(gc)[..., None]
    kg = k * jnp.exp(gc[..., -1:] - gc)[..., None]
    glast = jnp.exp(gc[..., -1])

    def step(state, xs):
        u_c, w_c, attn_c, qg_c, kg_c, gl_c = xs
        v_new = u_c - jnp.einsum('bhcd,bhde->bhce', w_c, state)
        o = (jnp.einsum('bhcd,bhde->bhce', qg_c, state)
             + jnp.einsum('bhcs,bhse->bhce', attn_c, v_new))
        state = state * gl_c[..., None, None] + jnp.einsum('bhcd,bhce->bhde', kg_c, v_new)
        return state, o

    xs = tuple(jnp.moveaxis(a, 2, 0) for a in (u, w, attn, qg, kg, glast))
    s0 = jnp.zeros((bsz, nh, dk, dv), F32)
    _, o = lax.scan(step, s0, xs)
    return jnp.moveaxis(o, 0, 2).reshape(bsz, nh, L, dv)


def _gdn_mixer(q, k, v, a, b, gate, conv_w, a_log, dt_bias, norm_w):
    bsz, L, _ = q.shape
    qkv = jax.nn.silu(_causal_conv(jnp.concatenate([q, k, v], axis=-1), conv_w))
    q, k, v = jnp.split(qkv, 3, axis=-1)

    def heads(t):
        return t.reshape(bsz, L, GDN_HEADS, HEAD_DIM).transpose(0, 2, 1, 3)

    qh, kh = _l2norm(heads(q)), _l2norm(heads(k))
    vh = heads(v)
    beta = jax.nn.sigmoid(b).transpose(0, 2, 1)
    g = (-jnp.exp(a_log) * jax.nn.softplus(a + dt_bias)).transpose(0, 2, 1)
    o = _gated_delta_chunked(qh, kh, vh, g, beta).transpose(0, 2, 1, 3)
    o = _rmsnorm(o, norm_w) * jax.nn.silu(gate).reshape(bsz, L, GDN_HEADS, HEAD_DIM)
    return o.reshape(bsz, L, GDN_WIDTH)


def _banded_attention(q, k, v, window):
    nn_, s_len, nh, d = q.shape
    bq = math.gcd(s_len, DSA_BLOCK)
    nb = s_len // bq
    kp = jnp.pad(k, ((0, 0), (window, 0), (0, 0), (0, 0)))
    vp = jnp.pad(v, ((0, 0), (window, 0), (0, 0), (0, 0)))
    starts = jnp.arange(nb) * bq
    kidx = starts[:, None] + jnp.arange(bq + window)[None, :]
    kb, vb = kp[:, kidx], vp[:, kidx]
    qb = q.reshape(nn_, nb, bq, nh, d)
    s = jnp.einsum('nbqhd,nbkhd->nbhqk', qb, kb) * (d ** -0.5)
    qi = jnp.arange(bq)[:, None]
    kj = jnp.arange(bq + window)[None, :]
    band = (kj >= qi) & (kj <= qi + window)
    valid = kj[None] >= (window - starts)[:, None, None]
    mask = band[None] & valid
    s = jnp.where(mask[None, :, None], s, -jnp.inf)
    mx = jnp.max(s, axis=-1, keepdims=True)
    p = jnp.exp(s - mx)
    den = jnp.sum(p, axis=-1, keepdims=True)
    o = jnp.einsum('nbhqk,nbkhd->nbqhd', p / den, vb)
    lse = (mx + jnp.log(den))[..., 0].transpose(0, 1, 3, 2)
    return o.reshape(nn_, s_len, nh, d), lse.reshape(nn_, s_len, nh)


def _dsa_mixer(q, k, v, positions, q_norm, k_norm):
    bsz, L, _ = q.shape
    hg = DSA_HEADS_PER_GROUP
    shp = (bsz, L, DSA_HEADS, HEAD_DIM)
    qh = _rope(_rmsnorm(q.reshape(shp), q_norm), positions)
    kh = _rope(_rmsnorm(k.reshape(shp), k_norm), positions)
    vh = v.reshape(shp)
    outs, lses = [], []
    for gi, (window, dil) in enumerate(DSA_PATTERNS):
        lo, hi = gi * hg, (gi + 1) * hg
        ls = L // dil

        def to_sub(t):
            t = t[:, :, lo:hi].reshape(bsz, ls, dil, hg, HEAD_DIM).transpose(0, 2, 1, 3, 4)
            return t.reshape(bsz * dil, ls, hg, HEAD_DIM)

        o, lse = _banded_attention(to_sub(qh), to_sub(kh), to_sub(vh), window // dil)
        outs.append(o.reshape(bsz, dil, ls, hg, HEAD_DIM).transpose(0, 2, 1, 3, 4)
                    .reshape(bsz, L, hg, HEAD_DIM))
        lses.append(lse.reshape(bsz, dil, ls, hg).transpose(0, 2, 1, 3).reshape(bsz, L, hg))
    wts = jax.nn.softmax(jnp.stack(lses, axis=0), axis=0)
    o = jnp.einsum('gblh,gblhd->blhd', wts, jnp.stack(outs, axis=0))
    return o.reshape(bsz, L, DSA_OUT)


def _fox_mixer(q, k, v, f_logit, q_norm, k_norm, f_bias):
    bsz, L, _ = q.shape
    shp = (bsz, L, FOX_HEADS, HEAD_DIM)
    qh = _rmsnorm(q.reshape(shp), q_norm) * (HEAD_DIM ** -0.5)
    kh = _rmsnorm(k.reshape(shp), k_norm)
    vh = v.reshape(shp)
    logf = jax.nn.log_sigmoid(f_logit + f_bias)
    cum = jnp.cumsum(logf, axis=1).transpose(0, 2, 1)
    nb = L // FOX_BLOCK
    qb = qh.reshape(bsz, nb, FOX_BLOCK, FOX_HEADS, HEAD_DIM).transpose(1, 0, 2, 3, 4)
    cq = cum.reshape(bsz, FOX_HEADS, nb, FOX_BLOCK).transpose(2, 0, 1, 3)
    kpos = jnp.arange(L)

    def block(args):
        q_i, c_i, b_i = args
        qpos = b_i * FOX_BLOCK + jnp.arange(FOX_BLOCK)
        s = jnp.einsum('bqhd,bkhd->bhqk', q_i, kh) + c_i[..., None] - cum[:, :, None, :]
        s = jnp.where(kpos[None, :] <= qpos[:, None], s, -jnp.inf)
        p = jax.nn.softmax(s, axis=-1)
        return jnp.einsum('bhqk,bkhd->bqhd', p, vh)

    o = lax.map(block, (qb, cq, jnp.arange(nb)))
    return o.transpose(1, 0, 2, 3, 4).reshape(bsz, L, FOX_WIDTH)


def _pack_w_in(w):
    pad = jnp.zeros((D_MODEL, MIX_WIDTH - OFF_SMALL - 24), w.dtype)
    w_mix = jnp.concatenate(
        [w[:, 0:4096], w[:, 4112:12816], w[:, 4096:4112], w[:, 12816:12824], pad], axis=1)
    return w_mix.astype(BF16), w[:, 12824:].astype(BF16)


def kernel(x, c, positions, w_mod, b_mod, norm_mix_w, norm_ffn_w, w_in, s5_lambda_re, s5_lambda_im, s5_b_re, s5_b_im, s5_c_re, s5_c_im, s5_d, s5_log_dt, s5_w_glu, gdn_conv_w, gdn_a_log, gdn_dt_bias, gdn_norm_w, dsa_q_norm, dsa_k_norm, fox_q_norm, fox_k_norm, fox_f_bias, w_br_s5, w_br_gdn, w_br_dsa, w_br_fox, w_out, ffn_w_gate, ffn_w_up, ffn_w_down, moe_router, moe_w_gate, moe_w_up, moe_w_down):
    bsz, L, d = x.shape
    depth = w_in.shape[0]
    m = bsz * L
    xf = x.reshape(m, d)

    c_pad = jnp.zeros((8, d), F32).at[:bsz].set(c)
    cond = _cond_proj(c_pad, w_mod)[:bsz].reshape(bsz, N_MOD, d)

    for l in range(depth):
        mod = cond + b_mod[l]
        shift1, scale1, gate1, shift2, scale2, gate2 = (mod[:, i][:, None, :] for i in range(N_MOD))

        h = _norm_modulate(xf, norm_mix_w[l], scale1, shift1, rows_per_batch=L)
        w_mix, w_gate = _pack_w_in(w_in[l])
        z = _matmul(h, w_mix, out_dtype=F32).reshape(bsz, L, MIX_WIDTH)
        br_gate = _matmul(h, w_gate, out_dtype=BF16)

        def seg(off, width):
            return z[:, :, off:off + width]

        y_s5 = _s5_mixer(seg(OFF_S5, S5_WIDTH), s5_lambda_re[l], s5_lambda_im[l], s5_b_re[l],
                         s5_b_im[l], s5_c_re[l], s5_c_im[l], s5_d[l], s5_log_dt[l], s5_w_glu[l])
        y_gdn = _gdn_mixer(seg(OFF_GQ, 1024), seg(OFF_GK, 1024), seg(OFF_GV, 1024),
                           seg(OFF_SMALL, 8), seg(OFF_SMALL + 8, 8), seg(OFF_GGATE, 1024),
                           gdn_conv_w[l], gdn_a_log[l], gdn_dt_bias[l], gdn_norm_w[l])
        y_dsa = _dsa_mixer(seg(OFF_DQ, 1536), seg(OFF_DK, 1536), seg(OFF_DV, 1536), positions,
                           dsa_q_norm[l], dsa_k_norm[l])
        y_fox = _fox_mixer(seg(OFF_FQ, 1024), seg(OFF_FK, 1024), seg(OFF_FV, 1024),
                           seg(OFF_SMALL + 16, 8), fox_q_norm[l], fox_k_norm[l], fox_f_bias[l])
        ys = [y.reshape(m, -1).astype(BF16) for y in (y_s5, y_gdn, y_dsa, y_fox)]
        ws = [w[l].astype(BF16) for w in (w_br_s5, w_br_gdn, w_br_dsa, w_br_fox)]
        merged = _merge(ys, ws, br_gate)
        xf = _matmul(merged, w_out[l].astype(BF16), out_dtype=F32, x=xf, gate=gate1,
                     rows_per_batch=L)

        j = l // 2
        if l % 2 == 0:
            h2 = _norm_modulate(xf, norm_ffn_w[l], scale2, shift2, rows_per_batch=L)
            padn = ((0, 0), (0, FFN_PAD - FFN_DIM))
            wg = jnp.pad(ffn_w_gate[j], padn).astype(BF16)
            wu = jnp.pad(ffn_w_up[j], padn).astype(BF16)
            wd = jnp.pad(ffn_w_down[j], ((0, FFN_PAD - FFN_DIM), (0, 0))).astype(BF16)
            act = _swiglu_up(h2, wg, wu)
            xf = _matmul(act, wd, out_dtype=F32, tk=1024, x=xf, gate=gate2, rows_per_batch=L)
        else:
            h2f = _norm_modulate(xf, norm_ffn_w[l], scale2, shift2, rows_per_batch=L,
                                 out_dtype=F32)
            logits = jnp.dot(h2f, moe_router[j], precision=lax.Precision.HIGHEST)
            top_v, top_i = lax.top_k(logits, TOP_K)
            top_w = jax.nn.softmax(top_v, axis=-1)
            gate_e = jnp.sum(jax.nn.one_hot(top_i, N_EXPERTS, dtype=F32) * top_w[..., None], axis=-2)
            row_scale = jnp.repeat(gate_e, 128, axis=1)
            act = _swiglu_up(h2f.astype(BF16), moe_w_gate[j].astype(BF16), moe_w_up[j].astype(BF16),
                             row_scale=row_scale, tiles_per_expert=EXPERT_DIM // 512)
            wd = moe_w_down[j].reshape(N_EXPERTS * EXPERT_DIM, d).astype(BF16)
            xf = _matmul(act, wd, out_dtype=F32, tk=2048, x=xf, gate=gate2, rows_per_batch=L)

    return xf.reshape(bsz, L, d)
```
